```python
import math
import jax
import jax.numpy as jnp
from jax import lax
import numpy as np

D_MODEL = 1024
BATCH = 8
SEQ = 4096
DEPTH = 1

GRID_W = 64
CTX_LEN = 256
NH_M = 4
DQK_M = D_MODEL // 8
DV_M = D_MODEL // 4
MLSTM_CHUNK = 128
NH_D = 8
DH_D = D_MODEL // 16
Q_BLOCK = 128
ROPE_BASE = 10000.0
N_EXPERTS = 32
TOP_K = 4
D_FF = D_MODEL
SWIGLU_LIMIT = 7.0
SWIGLU_ALPHA = 1.702
EXPERT_BLOCK = 128
NORM_EPS = 1e-6
SPLIT_SIZES = (NH_M * DQK_M, NH_M * DQK_M, NH_M * DV_M, 4 * NH_M, NH_M * DV_M,
               NH_D * 2 * DH_D, NH_D * 2 * DH_D, NH_D * 2 * DH_D, D_MODEL, D_MODEL)
IN_COLS = sum(SPLIT_SIZES)

kernel_name = 'hybrid_mlstm_diffattn_moe_dit'


def rmsnorm(x, g):
    xf = x.astype(jnp.float32)
    y = xf * lax.rsqrt(jnp.mean(xf * xf, axis=-1, keepdims=True) + NORM_EPS)
    return (y * g.astype(jnp.float32)).astype(x.dtype)


def modulate(u, shift, scale):
    return u * (1 + scale) + shift


def split_proj(u, w_in):
    return jnp.split(u @ w_in, np.cumsum(SPLIT_SIZES)[:-1].tolist(), axis=-1)


def axial_rope_tables(n_tok):
    rows = n_tok // GRID_W
    row = jnp.repeat(jnp.arange(rows, dtype=jnp.float32), GRID_W)
    col = jnp.tile(jnp.arange(GRID_W, dtype=jnp.float32), rows)
    n_freq = DH_D // 4
    inv_freq = ROPE_BASE ** (-jnp.arange(n_freq, dtype=jnp.float32) / n_freq)
    ang_r = (row[:, None] * inv_freq)[:, None, None, :]
    ang_c = (col[:, None] * inv_freq)[:, None, None, :]
    return (jnp.cos(ang_r), jnp.sin(ang_r), jnp.cos(ang_c), jnp.sin(ang_c))


def rotate_pairs(x, cos, sin):
    x1, x2 = jnp.split(x, 2, axis=-1)
    cos = cos.astype(x.dtype)
    sin = sin.astype(x.dtype)
    return jnp.concatenate([x1 * cos - x2 * sin, x2 * cos + x1 * sin], axis=-1)


def apply_axial_rope(x, tables):
    cos_r, sin_r, cos_c, sin_c = tables
    half = DH_D // 2
    return jnp.concatenate([rotate_pairs(x[..., :half], cos_r, sin_r),
                            rotate_pairs(x[..., half:], cos_c, sin_c)], axis=-1)


def mlstm_chunkwise(q, k, v, i_pre, logf, state):
    b_, h_, t_, _ = q.shape
    L = MLSTM_CHUNK
    nc = t_ // L

    def chunks(a):
        return jnp.moveaxis(a.reshape(a.shape[:2] + (nc, L) + a.shape[3:]), 2, 0)

    causal = jnp.tril(jnp.ones((L, L), dtype=bool))

    def step(carry, xs):
        C, n, m = carry
        qc, kc, vc, ic, fc = xs
        bcum = jnp.cumsum(fc, axis=-1)
        log_d = jnp.where(causal, bcum[..., :, None] - bcum[..., None, :] + ic[..., None, :], -jnp.inf)
        inter = bcum + m[..., None]
        m_t = jnp.maximum(inter, jnp.max(log_d, axis=-1))
        s = jnp.einsum('bhtk,bhsk->bhts', qc, kc) * jnp.exp(log_d - m_t[..., None])
        w_inter = jnp.exp(inter - m_t)
        num = w_inter[..., None] * jnp.einsum('bhtk,bhkv->bhtv', qc, C) + jnp.einsum('bhts,bhsv->bhtv', s, vc)
        den = w_inter * jnp.einsum('bhtk,bhk->bht', qc, n) + jnp.sum(s, axis=-1)
        h = num / jnp.maximum(jnp.abs(den), jnp.exp(-m_t))[..., None]
        b_last = bcum[..., -1]
        log_w = b_last[..., None] - bcum + ic
        m_new = jnp.maximum(b_last + m, jnp.max(log_w, axis=-1))
        w = jnp.exp(log_w - m_new[..., None])
        decay = jnp.exp(b_last + m - m_new)
        C_new = decay[..., None, None] * C + jnp.einsum('bhsk,bhsv->bhkv', kc * w[..., None], vc)
        n_new = decay[..., None] * n + jnp.einsum('bhs,bhsk->bhk', w, kc)
        return (C_new, n_new, m_new), h

    state, h = lax.scan(step, state, (chunks(q), chunks(k), chunks(v), chunks(i_pre), chunks(logf)))
    return jnp.moveaxis(h, 0, 2).reshape(b_, h_, t_, v.shape[-1]), state


def mlstm_zero_state(batch):
    return (jnp.zeros((batch, NH_M, DQK_M, DV_M), jnp.float32),
            jnp.zeros((batch, NH_M, DQK_M), jnp.float32),
            jnp.zeros((batch, NH_M), jnp.float32))


def mlstm_prepare(p, gate_b):
    b, t = p[0].shape[:2]

    def heads(a, d):
        return a.reshape(b, t, NH_M, d).transpose(0, 2, 1, 3).astype(jnp.float32)

    q = heads(p[0], DQK_M) * DQK_M ** -0.5
    k = heads(p[1], DQK_M)
    v = heads(p[2], DV_M)
    g = (p[3] + gate_b).astype(jnp.float32).reshape(b, t, 4, NH_M).transpose(2, 0, 3, 1)
    return q, k, v, g


def mlstm_direction(q, k, v, i_pre, f_pre, state, reverse):
    logf = jax.nn.log_sigmoid(f_pre)
    if reverse:
        q, k, v, i_pre, logf = (jnp.flip(a, axis=2) for a in (q, k, v, i_pre, logf))
    h, state = mlstm_chunkwise(q, k, v, i_pre, logf, state)
    if reverse:
        h = jnp.flip(h, axis=2)
    return h, state


def mlstm_output(h, o_raw, g):
    b, t = o_raw.shape[:2]
    hn = rmsnorm(h.transpose(0, 2, 1, 3), g)
    o = jax.nn.sigmoid(o_raw.astype(jnp.float32)).reshape(b, t, NH_M, DV_M)
    return (hn * o).reshape(b, t, NH_M * DV_M).astype(o_raw.dtype)


def diff_prepare(p, tables):
    b, t = p[5].shape[:2]
    q = p[5].reshape(b, t, NH_D, 2, DH_D)
    k = p[6].reshape(b, t, NH_D, 2, DH_D)
    if tables is not None:
        q = apply_axial_rope(q, tables)
        k = apply_axial_rope(k, tables)
    v = p[7].reshape(b, t, NH_D, 2 * DH_D)
    return q.transpose(0, 2, 3, 1, 4), k.transpose(0, 2, 3, 1, 4), v.transpose(0, 2, 1, 3)


def diff_attend(q, k, v, lam):
    s = jnp.einsum('bhmqd,bhmkd->bhmqk', q, k).astype(jnp.float32) * DH_D ** -0.5
    p = jax.nn.softmax(s, axis=-1)
    a = p[:, :, 0] - lam * p[:, :, 1]
    return jnp.einsum('bhqk,bhkd->bhqd', a.astype(v.dtype), v)


def diff_output(o, g, lam_init):
    b, h, t, d = o.shape
    on = rmsnorm(o.transpose(0, 2, 1, 3), g) * (1.0 - lam_init)
    return on.reshape(b, t, h * d)


def merge_branches(p, a_out, b_out, w_a, w_b, w_o):
    y = jax.nn.sigmoid(p[8]) * (a_out @ w_a) + jax.nn.sigmoid(p[9]) * (b_out @ w_b)
    return y @ w_o


def token_mixer(u, uc, tables, w_in, gate_b, m_norm_g, lam_p, d_norm_g, w_a, w_b, w_o, lam_init, need_ctx):
    pl = split_proj(u, w_in)
    pc = split_proj(uc, w_in)
    ql, kl, vl, gl = mlstm_prepare(pl, gate_b)
    qc, kc, vc, gc = mlstm_prepare(pc, gate_b)
    st0 = mlstm_zero_state(uc.shape[0])
    h_cf, st_f = mlstm_direction(qc, kc, vc, gc[0], gc[1], st0, False)
    h_cb, st_b = mlstm_direction(qc, kc, vc, gc[2], gc[3], st0, True)
    h_lf, _ = mlstm_direction(ql, kl, vl, gl[0], gl[1], st_f, False)
    h_lb, _ = mlstm_direction(ql, kl, vl, gl[2], gl[3], st_b, True)
    a_lat = mlstm_output(h_lf + h_lb, pl[4], m_norm_g)
    lp = lam_p.astype(jnp.float32)
    lam = jnp.exp(jnp.sum(lp[0] * lp[1])) - jnp.exp(jnp.sum(lp[2] * lp[3])) + lam_init
    qdl, kdl, vdl = diff_prepare(pl, tables)
    qdc, kdc, vdc = diff_prepare(pc, None)
    k_all = jnp.concatenate([kdc, kdl], axis=3)
    v_all = jnp.concatenate([vdc, vdl], axis=2)
    b, h, _, t, dh = qdl.shape
    n_blk = t // Q_BLOCK
    q_blocks = jnp.moveaxis(qdl.reshape(b, h, 2, n_blk, Q_BLOCK, dh), 3, 0)
    o_blocks = lax.map(lambda qb: diff_attend(qb, k_all, v_all, lam), q_blocks)
    o_lat = jnp.moveaxis(o_blocks, 0, 2).reshape(b, h, t, 2 * dh)
    b_lat = diff_output(o_lat, d_norm_g, lam_init)
    y = merge_branches(pl, a_lat, b_lat, w_a, w_b, w_o)
    if not need_ctx:
        return y, None
    a_ctx = mlstm_output(h_cf + h_cb, pc[4], m_norm_g)
    b_ctx = diff_output(diff_attend(qdc, kdc, vdc, lam), d_norm_g, lam_init)
    return y, merge_branches(pc, a_ctx, b_ctx, w_a, w_b, w_o)


def moe_ffn(u, router_w, router_b, w1, b1, w2, b2):
    b, t, d = u.shape
    n_tok = b * t
    xf = u.reshape(n_tok, d)
    logits = (xf @ router_w + router_b).astype(jnp.float32)
    top_val, top_idx = lax.top_k(logits, TOP_K)
    gate = jax.nn.softmax(top_val, axis=-1).astype(u.dtype)
    n_asg = n_tok * TOP_K
    flat_e = top_idx.reshape(n_asg)
    flat_tok = jnp.arange(n_asg, dtype=jnp.int32) // TOP_K
    order = jnp.argsort(flat_e)
    sorted_e = flat_e[order]
    counts = jnp.bincount(flat_e, length=N_EXPERTS)
    padded = (counts + EXPERT_BLOCK - 1) // EXPERT_BLOCK * EXPERT_BLOCK
    pad_end = jnp.cumsum(padded)
    pad_start = pad_end - padded
    sorted_start = jnp.cumsum(counts) - counts
    dest = pad_start[sorted_e] + jnp.arange(n_asg) - sorted_start[sorted_e]
    n_buf = -(-n_asg // EXPERT_BLOCK) * EXPERT_BLOCK + N_EXPERTS * EXPERT_BLOCK
    buf_tok = jnp.zeros((n_buf,), jnp.int32).at[dest].set(flat_tok[order])
    buf_w = jnp.zeros((n_buf,), u.dtype).at[dest].set(gate.reshape(n_asg)[order])
    n_blk = n_buf // EXPERT_BLOCK
    blk_e = jnp.minimum(jnp.searchsorted(pad_end, jnp.arange(n_blk) * EXPERT_BLOCK, side='right'), N_EXPERTS - 1)

    def expert_block(args):
        tok, e = args
        hcat = xf[tok] @ w1[e] + b1[e]
        h_glu = jnp.minimum(hcat[:, :D_FF], SWIGLU_LIMIT)
        h_lin = jnp.clip(hcat[:, D_FF:], -SWIGLU_LIMIT, SWIGLU_LIMIT)
        act = h_glu * jax.nn.sigmoid(SWIGLU_ALPHA * h_glu) * (h_lin + 1)
        return act @ w2[e] + b2[e]

    y_buf = lax.map(expert_block, (buf_tok.reshape(n_blk, EXPERT_BLOCK), blk_e))
    y = jax.ops.segment_sum(y_buf.reshape(n_buf, d) * buf_w[:, None], buf_tok, num_segments=n_tok)
    return y.reshape(b, t, d)


def setup_inputs(seed: int = 0) -> dict:
    key = jax.random.key(seed)
    ks = jax.random.split(key, 24)

    def nrm(k, shape, scale):
        return jax.random.normal(k, shape, jnp.float32) * scale

    L = DEPTH
    gate_base = jnp.array([0.0, 3.0, 0.0, 3.0], jnp.float32)[None, :, None]
    return {
        'x': nrm(ks[0], (BATCH, SEQ, D_MODEL), 1.0),
        'c': nrm(ks[1], (BATCH, D_MODEL), 1.0),
        'ctx': nrm(ks[2], (BATCH, CTX_LEN, D_MODEL), 1.0),
        'c_ctx': nrm(ks[3], (D_MODEL,), 1.0),
        'ada_w': nrm(ks[4], (L, D_MODEL, 6 * D_MODEL), 0.5 * D_MODEL ** -0.5),
        'ada_b': nrm(ks[5], (L, 6 * D_MODEL), 0.02),
        'norm1_g': 1.0 + nrm(ks[6], (L, D_MODEL), 0.02),
        'norm2_g': 1.0 + nrm(ks[7], (L, D_MODEL), 0.02),
        'w_in': nrm(ks[8], (L, D_MODEL, IN_COLS), D_MODEL ** -0.5),
        'mlstm_gate_b': (gate_base + nrm(ks[9], (L, 4, NH_M), 0.1)).reshape(L, 4 * NH_M),
        'mlstm_norm_g': 1.0 + nrm(ks[10], (L, NH_M, DV_M), 0.02),
        'diff_lambda': nrm(ks[11], (L, 4, DH_D), 0.1),
        'diff_norm_g': 1.0 + nrm(ks[12], (L, NH_D, 2 * DH_D), 0.02),
        'w_branch_a': nrm(ks[13], (L, NH_M * DV_M, D_MODEL), (NH_M * DV_M) ** -0.5),
        'w_branch_b': nrm(ks[14], (L, NH_D * 2 * DH_D, D_MODEL), (NH_D * 2 * DH_D) ** -0.5),
        'w_out': nrm(ks[15], (L, D_MODEL, D_MODEL), D_MODEL ** -0.5),
        'router_w': nrm(ks[16], (L, D_MODEL, N_EXPERTS), D_MODEL ** -0.5),
        'router_b': nrm(ks[17], (L, N_EXPERTS), 0.01),
        'exp_w1': nrm(ks[18], (L, N_EXPERTS, D_MODEL, 2 * D_FF), D_MODEL ** -0.5),
        'exp_b1': nrm(ks[19], (L, N_EXPERTS, 2 * D_FF), 0.01),
        'exp_w2': nrm(ks[20], (L, N_EXPERTS, D_FF, D_MODEL), D_FF ** -0.5),
        'exp_b2': nrm(ks[21], (L, N_EXPERTS, D_MODEL), 0.01),
        'final_norm_g': 1.0 + nrm(ks[22], (D_MODEL,), 0.02),
    }


def reference(x, c, ctx, c_ctx, ada_w, ada_b, norm1_g, norm2_g, w_in, mlstm_gate_b, mlstm_norm_g,
              diff_lambda, diff_norm_g, w_branch_a, w_branch_b, w_out, router_w, router_b,
              exp_w1, exp_b1, exp_w2, exp_b2, final_norm_g):
    tables = axial_rope_tables(x.shape[1])
    xc = ctx
    for l in range(DEPTH):
        need_ctx = l < DEPTH - 1
        lam_init = 0.8 - 0.6 * math.exp(-0.3 * l)
        mod = jax.nn.silu(c) @ ada_w[l] + ada_b[l]
        sh1, sc1, g1, sh2, sc2, g2 = jnp.split(mod[:, None, :], 6, axis=-1)
        modc = jax.nn.silu(c_ctx) @ ada_w[l] + ada_b[l]
        sh1c, sc1c, g1c, sh2c, sc2c, g2c = jnp.split(modc, 6, axis=-1)
        u = modulate(rmsnorm(x, norm1_g[l]), sh1, sc1)
        uc = modulate(rmsnorm(xc, norm1_g[l]), sh1c, sc1c)
        y, yc = token_mixer(u, uc, tables, w_in[l], mlstm_gate_b[l], mlstm_norm_g[l], diff_lambda[l],
                            diff_norm_g[l], w_branch_a[l], w_branch_b[l], w_out[l], lam_init, need_ctx)
        x = x + g1 * y
        x = x + g2 * moe_ffn(modulate(rmsnorm(x, norm2_g[l]), sh2, sc2), router_w[l], router_b[l],
                             exp_w1[l], exp_b1[l], exp_w2[l], exp_b2[l])
        if need_ctx:
            xc = xc + g1c * yc
            xc = xc + g2c * moe_ffn(modulate(rmsnorm(xc, norm2_g[l]), sh2c, sc2c), router_w[l], router_b[l],
                                    exp_w1[l], exp_b1[l], exp_w2[l], exp_b2[l])
    return rmsnorm(x, final_norm_g)
```

```python
import functools
import math

import jax
import jax.numpy as jnp
from jax import lax
from jax.experimental import pallas as pl
from jax.experimental.pallas import tpu as pltpu

D_MODEL = 1024
GRID_W = 64
NH_M = 4
DQK_M = 128
DV_M = 256
CHUNK = 128
NH_D = 8
DH_D = 64
ROPE_BASE = 10000.0
N_EXPERTS = 32
TOP_K = 4
D_FF = 1024
SWIGLU_LIMIT = 7.0
SWIGLU_ALPHA = 1.702
NORM_EPS = 1e-6
LAM_INIT = 0.8 - 0.6 * math.exp(-0.3 * 0)

C_QM, C_KM, C_VM, C_OM = 0, 512, 1024, 2048
C_DQ, C_DK, C_DV, C_GA, C_GB = 3072, 4096, 5120, 6144, 7168
NP_MAIN = 8192
NP_GATE = 128
NP_ALL = NP_MAIN + NP_GATE

LANE = 128
EXPERT_BM = 256
VMEM_LIMIT = 56 * 1024 * 1024

BF16 = jnp.bfloat16
F32 = jnp.float32


def _cparams(sem):
    return pltpu.CompilerParams(dimension_semantics=sem, vmem_limit_bytes=VMEM_LIMIT)


def _ada_kernel(c_ref, w_ref, b_ref, o_ref):
    c = c_ref[...]
    s = c * jax.nn.sigmoid(c)
    o_ref[...] = jnp.dot(s, w_ref[...], preferred_element_type=F32,
                         precision=lax.Precision.HIGHEST) + b_ref[...]


def _ada(cvec, ada_w, ada_b):
    rows = cvec.shape[0]
    tn = 1024
    return pl.pallas_call(
        _ada_kernel,
        out_shape=jax.ShapeDtypeStruct((rows, 6 * D_MODEL), F32),
        grid=(6 * D_MODEL // tn,),
        in_specs=[pl.BlockSpec((rows, D_MODEL), lambda j: (0, 0)),
                  pl.BlockSpec((D_MODEL, tn), lambda j: (0, j)),
                  pl.BlockSpec((1, tn), lambda j: (0, j))],
        out_specs=pl.BlockSpec((rows, tn), lambda j: (0, j)),
        compiler_params=_cparams(("arbitrary",)),
        name="ada",
    )(cvec, ada_w, ada_b.reshape(1, -1))


def _inproj_kernel(x_ref, sh_ref, sc_ref, g_ref, w_ref, cos_ref, sin_ref, p_ref, gt_ref, *, rope, tn):
    x = x_ref[...]
    ms = jnp.mean(x * x, axis=-1, keepdims=True)
    u = (x * lax.rsqrt(ms + NORM_EPS) * g_ref[...]) * (1.0 + sc_ref[0]) + sh_ref[0]
    ub = u.astype(BF16)
    if rope:
        cos = cos_ref[...]
        sin = sin_ref[...]
        lane = lax.broadcasted_iota(jnp.int32, cos.shape, 1)
        lo = (lane % 32) < 16
    for c0 in range(0, NP_MAIN, tn):
        acc = jnp.dot(ub, w_ref[:, c0:c0 + tn], preferred_element_type=F32)
        if rope and C_DQ <= c0 < C_DV:
            for g0 in range(0, tn, LANE):
                xg = acc[:, g0:g0 + LANE]
                sw = jnp.where(lo, pltpu.roll(xg, LANE - 16, 1), pltpu.roll(xg, 16, 1))
                p_ref[:, c0 + g0:c0 + g0 + LANE] = (xg * cos + sw * sin).astype(BF16)
        else:
            p_ref[:, c0:c0 + tn] = acc.astype(BF16)
    gt_ref[...] = jnp.dot(ub, w_ref[:, NP_MAIN:], preferred_element_type=F32)


def _inproj(x2d, sh, sc, g, w_all, cos_t, sin_t, *, tokens_per_batch, tm, rope):
    n = x2d.shape[0]
    blocks_per_batch = tokens_per_batch // tm
    per_batch = sh.shape[0] > 1
    mod_map = (lambda i: (i // blocks_per_batch, 0, 0)) if per_batch else (lambda i: (0, 0, 0))
    tab_map = (lambda i: (i % blocks_per_batch, 0)) if rope else (lambda i: (0, 0))
    kern = functools.partial(_inproj_kernel, rope=rope, tn=512)
    return pl.pallas_call(
        kern,
        out_shape=(jax.ShapeDtypeStruct((n, NP_MAIN), BF16),
                   jax.ShapeDtypeStruct((n, NP_GATE), F32)),
        grid=(n // tm,),
        in_specs=[pl.BlockSpec((tm, D_MODEL), lambda i: (i, 0)),
                  pl.BlockSpec((1, 1, D_MODEL), mod_map),
                  pl.BlockSpec((1, 1, D_MODEL), mod_map),
                  pl.BlockSpec((1, D_MODEL), lambda i: (0, 0)),
                  pl.BlockSpec((D_MODEL, NP_ALL), lambda i: (0, 0), pipeline_mode=pl.Buffered(1)),
                  pl.BlockSpec((tm, LANE), tab_map),
                  pl.BlockSpec((tm, LANE), tab_map)],
        out_specs=(pl.BlockSpec((tm, NP_MAIN), lambda i: (i, 0)),
                   pl.BlockSpec((tm, NP_GATE), lambda i: (i, 0))),
        compiler_params=_cparams(("arbitrary",)),
        name="inproj_lat" if rope else "inproj_ctx",
    )(x2d, sh, sc, g, w_all, cos_t, sin_t)


def _log_sigmoid(x):
    return jnp.minimum(x, 0.0) - jnp.log1p(jnp.exp(-jnp.abs(x)))


def _cumsum_lanes(x):
    lane = lax.broadcasted_iota(jnp.int32, x.shape, 1)
    sh = 1
    while sh < x.shape[1]:
        x = x + jnp.where(lane >= sh, pltpu.roll(x, sh, 1), 0.0)
        sh *= 2
    return x


def _mlstm_kernel(gb_ref, gates_ref, qc_ref, kc_ref, vc_ref, ql_ref, kl_ref, vl_ref, ol_ref, ng_ref,
                  out_ref, hsum_ref, c_ref, gs_ref, *, n_ctx_chunks, n_lat_chunks):
    L = CHUNK
    h_idx = pl.program_id(1)
    nc = n_ctx_chunks + n_lat_chunks

    g = gates_ref[0, 0]
    for d in range(2):
        i_pre = g[2 * d] + gb_ref[(2 * d) * NH_M + h_idx]
        f_pre = g[2 * d + 1] + gb_ref[(2 * d + 1) * NH_M + h_idx]
        logf = _log_sigmoid(f_pre)
        pre = _cumsum_lanes(logf)
        tot = pre[:, L - 1:L]
        bc = pre if d == 0 else tot - pre + logf
        lw = tot - bc + i_pre
        gs_ref[d, 0] = bc
        gs_ref[d, 1] = i_pre - bc
        gs_ref[d, 2] = lw
        gs_ref[d, 3] = jnp.broadcast_to(tot, lw.shape)
        gs_ref[d, 4] = jnp.broadcast_to(jnp.max(lw, axis=1, keepdims=True), lw.shape)

    c_ref[...] = jnp.zeros(c_ref.shape, F32)

    r_i = lax.broadcasted_iota(jnp.int32, (L, L), 0)
    c_i = lax.broadcasted_iota(jnp.int32, (L, L), 1)
    eye = r_i == c_i
    masks = (c_i <= r_i, c_i >= r_i)

    def to_col(row):
        return jnp.sum(jnp.where(eye, row, 0.0), axis=1, keepdims=True)

    def step(d, gchunk, q, k, v, n, m, need_h):
        bc = gs_ref[d, 0, pl.ds(gchunk, 1), :]
        cm = gs_ref[d, 1, pl.ds(gchunk, 1), :]
        lw = gs_ref[d, 2, pl.ds(gchunk, 1), :]
        bt = gs_ref[d, 3, pl.ds(gchunk, 1), :][:, :1]
        ml = gs_ref[d, 4, pl.ds(gchunk, 1), :][:, :1]
        cmat = c_ref[d]
        h = None
        if need_h:
            a_col = to_col(bc)
            log_d = jnp.where(masks[d], a_col + cm, -jnp.inf)
            inter = a_col + m
            m_t = jnp.maximum(inter, jnp.max(log_d, axis=1, keepdims=True))
            dmat = jnp.exp(log_d - m_t)
            s = lax.dot_general(q, k, (((1,), (1,)), ((), ())), preferred_element_type=F32) * dmat
            w_inter = jnp.exp(inter - m_t)
            qc_ = jnp.dot(q, cmat.astype(BF16), preferred_element_type=F32)
            num = w_inter * qc_ + jnp.dot(s.astype(BF16), v, preferred_element_type=F32)
            qn = jnp.sum(q.astype(F32) * n, axis=1, keepdims=True)
            den = w_inter * qn + jnp.sum(s, axis=1, keepdims=True)
            h = num / jnp.maximum(jnp.abs(den), jnp.exp(-m_t))
        m_new = jnp.maximum(bt + m, ml)
        w_col = to_col(jnp.exp(lw - m_new))
        decay = jnp.exp(bt + m - m_new)
        kw = k.astype(F32) * w_col
        c_ref[d] = decay * cmat + lax.dot_general(kw.astype(BF16), v, (((0,), (0,)), ((), ())),
                                                  preferred_element_type=F32)
        n_new = decay * n + jnp.sum(kw, axis=0, keepdims=True)
        return h, n_new, m_new

    n_f = jnp.zeros((1, DQK_M), F32)
    n_b = jnp.zeros((1, DQK_M), F32)
    m_f = jnp.zeros((1, 1), F32)
    m_b = jnp.zeros((1, 1), F32)

    for j in range(n_ctx_chunks):
        jb = n_ctx_chunks - 1 - j
        _, n_f, m_f = step(0, j, qc_ref[0, j * L:(j + 1) * L, :], kc_ref[0, j * L:(j + 1) * L, :],
                           vc_ref[0, j * L:(j + 1) * L, :], n_f, m_f, False)
        _, n_b, m_b = step(1, jb, qc_ref[0, jb * L:(jb + 1) * L, :], kc_ref[0, jb * L:(jb + 1) * L, :],
                           vc_ref[0, jb * L:(jb + 1) * L, :], n_b, m_b, False)

    ng = ng_ref[0]

    def finish(rows, hs):
        ms = jnp.mean(hs * hs, axis=-1, keepdims=True)
        hn = hs * lax.rsqrt(ms + NORM_EPS) * ng
        o = jax.nn.sigmoid(ol_ref[0, rows, :].astype(F32))
        out_ref[0, rows, :] = (hn * o).astype(out_ref.dtype)

    def lat_pair(j, carry, second_half):
        n_f, m_f, n_b, m_b = carry
        jb = n_lat_chunks - 1 - j
        rf = pl.ds(pl.multiple_of(j * L, L), L)
        rb = pl.ds(pl.multiple_of(jb * L, L), L)
        h_f, n_f, m_f = step(0, n_ctx_chunks + j, ql_ref[0, rf, :], kl_ref[0, rf, :], vl_ref[0, rf, :],
                             n_f, m_f, True)
        h_b, n_b, m_b = step(1, n_ctx_chunks + jb, ql_ref[0, rb, :], kl_ref[0, rb, :], vl_ref[0, rb, :],
                             n_b, m_b, True)
        if second_half:
            finish(rf, hsum_ref[rf, :] + h_f)
            finish(rb, hsum_ref[rb, :] + h_b)
        else:
            hsum_ref[rf, :] = h_f
            hsum_ref[rb, :] = h_b
        return n_f, m_f, n_b, m_b

    half = n_lat_chunks // 2
    carry = lax.fori_loop(0, half, lambda j, c: lat_pair(j, c, False), (n_f, m_f, n_b, m_b))
    lax.fori_loop(half, n_lat_chunks, lambda j, c: lat_pair(j, c, True), carry)


def _mlstm(gate_b, gates_all, pc3, pl3, norm_g):
    b, t = pl3.shape[0], pl3.shape[1]
    ctx = pc3.shape[1]
    n_ctx_chunks, n_lat_chunks = ctx // CHUNK, t // CHUNK
    nc = n_ctx_chunks + n_lat_chunks
    kern = functools.partial(_mlstm_kernel, n_ctx_chunks=n_ctx_chunks, n_lat_chunks=n_lat_chunks)
    qo, ko, vo, oo = C_QM // DQK_M, C_KM // DQK_M, C_VM // DV_M, C_OM // DV_M
    return pl.pallas_call(
        kern,
        out_shape=jax.ShapeDtypeStruct((b, t, NH_M * DV_M), BF16),
        grid=(b, NH_M),
        in_specs=[pl.BlockSpec(memory_space=pltpu.SMEM),
                  pl.BlockSpec((1, 1, 4, nc, CHUNK), lambda i, h: (i, h, 0, 0, 0)),
                  pl.BlockSpec((1, ctx, DQK_M), lambda i, h: (i, 0, qo + h)),
                  pl.BlockSpec((1, ctx, DQK_M), lambda i, h: (i, 0, ko + h)),
                  pl.BlockSpec((1, ctx, DV_M), lambda i, h: (i, 0, vo + h)),
                  pl.BlockSpec((1, t, DQK_M), lambda i, h: (i, 0, qo + h)),
                  pl.BlockSpec((1, t, DQK_M), lambda i, h: (i, 0, ko + h)),
                  pl.BlockSpec((1, t, DV_M), lambda i, h: (i, 0, vo + h)),
                  pl.BlockSpec((1, t, DV_M), lambda i, h: (i, 0, oo + h)),
                  pl.BlockSpec((1, 1, DV_M), lambda i, h: (h, 0, 0))],
        out_specs=pl.BlockSpec((1, t, DV_M), lambda i, h: (i, 0, h)),
        scratch_shapes=[pltpu.VMEM((t, DV_M), F32),
                        pltpu.VMEM((2, DQK_M, DV_M), F32),
                        pltpu.VMEM((2, 5, nc, CHUNK), F32)],
        compiler_params=_cparams(("arbitrary", "arbitrary")),
        name="mlstm",
    )(gate_b, gates_all, pc3, pc3, pc3, pl3, pl3, pl3, pl3, norm_g.reshape(NH_M, 1, DV_M))


def _attn_kernel(lam_ref, g_ref, q_ref, kc_ref, kl_ref, vc_ref, vl_ref, o_ref, s_ref, *, ctx, t, kchunk):
    lp = lam_ref[...]
    lam = (jnp.exp(jnp.sum(lp[0:1] * lp[1:2], axis=1, keepdims=True))
           - jnp.exp(jnp.sum(lp[2:3] * lp[3:4], axis=1, keepdims=True)) + LAM_INIT)
    q = q_ref[0]
    tq = q.shape[0]
    lane = lax.broadcasted_iota(jnp.int32, q.shape, 1)
    zero = jnp.zeros_like(q)
    qm = (jnp.where(lane < DH_D, q, zero), jnp.where(lane >= DH_D, q, zero))
    segs = [(kc_ref, vc_ref, 0, 0, ctx)]
    for c0 in range(0, t, kchunk):
        segs.append((kl_ref, vl_ref, c0, ctx + c0, kchunk))

    mx = []
    for m in range(2):
        mrun = jnp.full((tq, 1), -jnp.inf, F32)
        for (k_ref, _, r0, s0, sz) in segs:
            s = lax.dot_general(qm[m], k_ref[0, r0:r0 + sz, :], (((1,), (1,)), ((), ())),
                                preferred_element_type=F32)
            s_ref[m, :, s0:s0 + sz] = s
            mrun = jnp.maximum(mrun, jnp.max(s, axis=1, keepdims=True))
        mx.append(mrun)
    ls = []
    for m in range(2):
        lrun = jnp.zeros((tq, 1), F32)
        for (_, _, r0, s0, sz) in segs:
            p = jnp.exp(s_ref[m, :, s0:s0 + sz] - mx[m])
            s_ref[m, :, s0:s0 + sz] = p
            lrun = lrun + jnp.sum(p, axis=1, keepdims=True)
        ls.append(lrun)
    c0_ = 1.0 / ls[0]
    c1_ = lam / ls[1]
    acc = jnp.zeros((tq, 2 * DH_D), F32)
    for (_, v_ref, r0, s0, sz) in segs:
        a = s_ref[0, :, s0:s0 + sz] * c0_ - s_ref[1, :, s0:s0 + sz] * c1_
        acc = acc + jnp.dot(a.astype(BF16), v_ref[0, r0:r0 + sz, :], preferred_element_type=F32)
    ms = jnp.mean(acc * acc, axis=-1, keepdims=True)
    on = acc * lax.rsqrt(ms + NORM_EPS) * g_ref[0] * (1.0 - LAM_INIT)
    o_ref[0] = on.astype(o_ref.dtype)


def _diffattn(lam_p, norm_g, pc3, pl3, *, tq=256, kchunk=512):
    b, t = pl3.shape[0], pl3.shape[1]
    ctx = pc3.shape[1]
    kchunk = min(kchunk, t)
    kern = functools.partial(_attn_kernel, ctx=ctx, t=t, kchunk=kchunk)
    qo, ko, vo = C_DQ // LANE, C_DK // LANE, C_DV // LANE
    return pl.pallas_call(
        kern,
        out_shape=jax.ShapeDtypeStruct((b, t, NH_D * 2 * DH_D), BF16),
        grid=(b, NH_D, t // tq),
        in_specs=[pl.BlockSpec((4, DH_D), lambda i, h, j: (0, 0)),
                  pl.BlockSpec((1, 1, 2 * DH_D), lambda i, h, j: (h, 0, 0)),
                  pl.BlockSpec((1, tq, LANE), lambda i, h, j: (i, j, qo + h)),
                  pl.BlockSpec((1, ctx, LANE), lambda i, h, j: (i, 0, ko + h)),
                  pl.BlockSpec((1, t, LANE), lambda i, h, j: (i, 0, ko + h)),
                  pl.BlockSpec((1, ctx, LANE), lambda i, h, j: (i, 0, vo + h)),
                  pl.BlockSpec((1, t, LANE), lambda i, h, j: (i, 0, vo + h))],
        out_specs=pl.BlockSpec((1, tq, LANE), lambda i, h, j: (i, j, h)),
        scratch_shapes=[pltpu.VMEM((2, tq, ctx + t), F32)],
        compiler_params=_cparams(("arbitrary", "arbitrary", "arbitrary")),
        name="diffattn",
    )(lam_p, norm_g.reshape(NH_D, 1, 2 * DH_D), pl3, pc3, pl3, pc3, pl3)


def _merge_kernel(a_ref, b_ref, ga_ref, gb_ref, x_ref, g1_ref, sh2_ref, sc2_ref, n2_ref,
                  wa_ref, wb_ref, wo_ref, rw_ref, rb_ref,
                  x1_ref, u2_ref, idx_ref, gate_ref, gatet_ref, cnt_ref):
    ya = jnp.dot(a_ref[...], wa_ref[...], preferred_element_type=F32)
    yb = jnp.dot(b_ref[...], wb_ref[...], preferred_element_type=F32)
    y = jax.nn.sigmoid(ga_ref[...].astype(F32)) * ya + jax.nn.sigmoid(gb_ref[...].astype(F32)) * yb
    y2 = jnp.dot(y.astype(BF16), wo_ref[...], preferred_element_type=F32)
    x1 = x_ref[...] + g1_ref[0] * y2
    x1_ref[...] = x1
    ms = jnp.mean(x1 * x1, axis=-1, keepdims=True)
    u2 = (x1 * lax.rsqrt(ms + NORM_EPS) * n2_ref[...]) * (1.0 + sc2_ref[0]) + sh2_ref[0]
    u2_ref[...] = u2
    logits = lax.dot_general(rw_ref[...], u2, (((1,), (1,)), ((), ())), preferred_element_type=F32,
                             precision=lax.Precision.HIGHEST) + rb_ref[...]
    tm = logits.shape[1]
    e_i = lax.broadcasted_iota(jnp.int32, logits.shape, 0)
    vals, cnt = [], jnp.zeros((N_EXPERTS, 1), F32)
    work = logits
    for k in range(TOP_K):
        mx = jnp.max(work, axis=0, keepdims=True)
        sel = jnp.min(jnp.where(work == mx, e_i, N_EXPERTS), axis=0, keepdims=True)
        hit = e_i == sel
        work = jnp.where(hit, -jnp.inf, work)
        idx_ref[k:k + 1, :] = sel
        vals.append(mx)
        cnt = cnt + jnp.sum(hit.astype(F32), axis=1, keepdims=True)
    ex = [jnp.exp(v - vals[0]) for v in vals]
    inv = 1.0 / (ex[0] + ex[1] + ex[2] + ex[3])
    r_i = lax.broadcasted_iota(jnp.int32, (LANE, tm), 0)
    gpad = jnp.zeros((LANE, tm), F32)
    for k in range(TOP_K):
        gk = ex[k] * inv
        gate_ref[k:k + 1, :] = gk
        gpad = jnp.where(r_i == k, gk, gpad)
    gatet_ref[...] = gpad.T[:, :8]

    @pl.when(pl.program_id(0) == 0)
    def _():
        cnt_ref[...] = jnp.zeros(cnt_ref.shape, F32)
    cnt_ref[...] += cnt


def _merge(a_lat, b_lat, p_lat, x2d, g1, sh2, sc2, n2g, wa, wb, wo, rwt, rb, *, tokens_per_batch, tm=512):
    n = x2d.shape[0]
    bpb = tokens_per_batch // tm
    mod_map = lambda i: (i // bpb, 0, 0)
    row = lambda i: (i, 0)
    const = lambda i: (0, 0)
    return pl.pallas_call(
        _merge_kernel,
        out_shape=(jax.ShapeDtypeStruct((n, D_MODEL), F32),
                   jax.ShapeDtypeStruct((n, D_MODEL), F32),
                   jax.ShapeDtypeStruct((TOP_K, n), jnp.int32),
                   jax.ShapeDtypeStruct((TOP_K, n), F32),
                   jax.ShapeDtypeStruct((n, 8), F32),
                   jax.ShapeDtypeStruct((N_EXPERTS, 1), F32)),
        grid=(n // tm,),
        in_specs=[pl.BlockSpec((tm, D_MODEL), row),
                  pl.BlockSpec((tm, D_MODEL), row),
                  pl.BlockSpec((tm, D_MODEL), lambda i: (i, C_GA // D_MODEL)),
                  pl.BlockSpec((tm, D_MODEL), lambda i: (i, C_GB // D_MODEL)),
                  pl.BlockSpec((tm, D_MODEL), row),
                  pl.BlockSpec((1, 1, D_MODEL), mod_map),
                  pl.BlockSpec((1, 1, D_MODEL), mod_map),
                  pl.BlockSpec((1, 1, D_MODEL), mod_map),
                  pl.BlockSpec((1, D_MODEL), const),
                  pl.BlockSpec((D_MODEL, D_MODEL), const),
                  pl.BlockSpec((D_MODEL, D_MODEL), const),
                  pl.BlockSpec((D_MODEL, D_MODEL), const),
                  pl.BlockSpec((N_EXPERTS, D_MODEL), const),
                  pl.BlockSpec((N_EXPERTS, 1), const)],
        out_specs=(pl.BlockSpec((tm, D_MODEL), row),
                   pl.BlockSpec((tm, D_MODEL), row),
                   pl.BlockSpec((TOP_K, tm), lambda i: (0, i)),
                   pl.BlockSpec((TOP_K, tm), lambda i: (0, i)),
                   pl.BlockSpec((tm, 8), row),
                   pl.BlockSpec((N_EXPERTS, 1), const)),
        compiler_params=_cparams(("arbitrary",)),
        name="merge",
    )(a_lat, b_lat, p_lat, p_lat, x2d, g1, sh2, sc2, n2g, wa, wb, wo, rwt, rb)


def _pos_kernel(idx_ref, base_ref, pos_ref, run_ref):
    @pl.when(pl.program_id(0) == 0)
    def _():
        run_ref[...] = jnp.zeros(run_ref.shape, F32)
    tn = idx_ref.shape[1]
    upper = (lax.broadcasted_iota(jnp.int32, (tn, tn), 0)
             < lax.broadcasted_iota(jnp.int32, (tn, tn), 1)).astype(BF16)
    e_i = lax.broadcasted_iota(jnp.int32, (N_EXPERTS, tn), 0)
    run = run_ref[...]
    base = base_ref[...]
    for k in range(TOP_K):
        hit = e_i == idx_ref[k:k + 1, :]
        onehot = hit.astype(F32)
        before = jnp.dot(onehot.astype(BF16), upper, preferred_element_type=F32)
        dest = jnp.sum(jnp.where(hit, base + run + before, 0.0), axis=0, keepdims=True)
        pos_ref[k:k + 1, :] = dest.astype(jnp.int32)
        run = run + jnp.sum(onehot, axis=1, keepdims=True)
    run_ref[...] = run


def _positions(idx, base, *, tn=1024):
    n = idx.shape[1]
    tn = min(tn, n)
    return pl.pallas_call(
        _pos_kernel,
        out_shape=jax.ShapeDtypeStruct((TOP_K, n), jnp.int32),
        grid=(n // tn,),
        in_specs=[pl.BlockSpec((TOP_K, tn), lambda i: (0, i)),
                  pl.BlockSpec((N_EXPERTS, 1), lambda i: (0, 0))],
        out_specs=pl.BlockSpec((TOP_K, tn), lambda i: (0, i)),
        scratch_shapes=[pltpu.VMEM((N_EXPERTS, 1), F32)],
        compiler_params=_cparams(("arbitrary",)),
        name="positions",
    )(idx, base)


def _dispatch_kernel(last_ref, nu_ref, pos_ref, u_ref, xs_ref, zbuf, zsem, sem, *, n_blk):
    tm = u_ref.shape[0]

    def zero_block(j):
        return pltpu.make_async_copy(zbuf, xs_ref.at[pl.ds(pl.multiple_of(j * EXPERT_BM, EXPERT_BM), EXPERT_BM)],
                                     zsem)

    @pl.when(pl.program_id(0) == 0)
    def _():
        zbuf[...] = jnp.zeros(zbuf.shape, zbuf.dtype)
        n_used = nu_ref[0]

        def start_last(e, c):
            zero_block(last_ref[e]).start()
            return c

        def start_tail(j, c):
            zero_block(j).start()
            return c

        def wait_one(j, c):
            zero_block(0).wait()
            return c

        lax.fori_loop(0, N_EXPERTS, start_last, 0)
        lax.fori_loop(n_used, n_blk, start_tail, 0)
        lax.fori_loop(0, N_EXPERTS + n_blk - n_used, wait_one, 0)

    def body(r, c):
        for k in range(TOP_K):
            pltpu.make_async_copy(u_ref.at[pl.ds(r, 1)], xs_ref.at[pl.ds(pos_ref[k, r], 1)], sem).start()
        return c

    lax.fori_loop(0, tm, body, 0)
    for k in range(TOP_K):
        pltpu.make_async_copy(u_ref, xs_ref.at[pl.ds(0, tm)], sem).wait()


def _dispatch(last_blk, n_used, pos, u2, n_buf, *, tm=256):
    n = u2.shape[0]
    kern = functools.partial(_dispatch_kernel, n_blk=n_buf // EXPERT_BM)
    return pl.pallas_call(
        kern,
        out_shape=jax.ShapeDtypeStruct((n_buf, D_MODEL), u2.dtype),
        grid_spec=pltpu.PrefetchScalarGridSpec(
            num_scalar_prefetch=2, grid=(n // tm,),
            in_specs=[pl.BlockSpec((TOP_K, tm), lambda i, last, nu: (0, i), memory_space=pltpu.SMEM),
                      pl.BlockSpec((tm, D_MODEL), lambda i, last, nu: (i, 0))],
            out_specs=pl.BlockSpec(memory_space=pl.ANY),
            scratch_shapes=[pltpu.VMEM((EXPERT_BM, D_MODEL), u2.dtype),
                            pltpu.SemaphoreType.DMA(()), pltpu.SemaphoreType.DMA(())]),
        compiler_params=_cparams(("arbitrary",)),
        name="dispatch",
    )(last_blk, n_used, pos, u2)


def _expert_kernel(be_ref, bi_ref, nu_ref, x_ref, w1_ref, b1_ref, w2_ref, b2_ref, y_ref):
    @pl.when(pl.program_id(0) < nu_ref[0])
    def _():
        hcat = jnp.dot(x_ref[...].astype(BF16), w1_ref[0], preferred_element_type=F32) + b1_ref[0]
        h_glu = jnp.minimum(hcat[:, :D_FF], SWIGLU_LIMIT)
        h_lin = jnp.clip(hcat[:, D_FF:], -SWIGLU_LIMIT, SWIGLU_LIMIT)
        act = h_glu * jax.nn.sigmoid(SWIGLU_ALPHA * h_glu) * (h_lin + 1.0)
        y_ref[...] = jnp.dot(act.astype(BF16), w2_ref[0], preferred_element_type=F32) + b2_ref[0]

    @pl.when(pl.program_id(0) >= nu_ref[0])
    def _():
        y_ref[...] = jnp.zeros(y_ref.shape, y_ref.dtype)


def _experts(blk_e, blk_i, n_used, xs, w1, b1, w2, b2):
    n_buf = xs.shape[0]
    n_blk = n_buf // EXPERT_BM
    return pl.pallas_call(
        _expert_kernel,
        out_shape=jax.ShapeDtypeStruct((n_buf, D_MODEL), F32),
        grid_spec=pltpu.PrefetchScalarGridSpec(
            num_scalar_prefetch=3, grid=(n_blk,),
            in_specs=[pl.BlockSpec((EXPERT_BM, D_MODEL), lambda j, be, bi, nu: (bi[j], 0)),
                      pl.BlockSpec((1, D_MODEL, 2 * D_FF), lambda j, be, bi, nu: (be[j], 0, 0)),
                      pl.BlockSpec((1, 1, 2 * D_FF), lambda j, be, bi, nu: (be[j], 0, 0)),
                      pl.BlockSpec((1, D_FF, D_MODEL), lambda j, be, bi, nu: (be[j], 0, 0)),
                      pl.BlockSpec((1, 1, D_MODEL), lambda j, be, bi, nu: (be[j], 0, 0))],
            out_specs=pl.BlockSpec((EXPERT_BM, D_MODEL), lambda j, be, bi, nu: (j, 0))),
        compiler_params=_cparams(("arbitrary",)),
        name="experts",
    )(blk_e, blk_i, n_used, xs, w1, b1, w2, b2)


def _combine_kernel(pos_ref, gt_ref, x1_ref, g2_ref, fg_ref, y_ref, o_ref, ybuf, sem):
    tm = x1_ref.shape[0]

    def body(r, c):
        for k in range(TOP_K):
            pltpu.make_async_copy(y_ref.at[pl.ds(pos_ref[k, r], 1)], ybuf.at[pl.ds(k * tm + r, 1)], sem).start()
        return c

    lax.fori_loop(0, tm, body, 0)
    pltpu.make_async_copy(y_ref.at[pl.ds(0, TOP_K * tm)], ybuf, sem).wait()
    gt = gt_ref[...]
    moe = gt[:, 0:1] * ybuf[0:tm, :]
    for k in range(1, TOP_K):
        moe = moe + gt[:, k:k + 1] * ybuf[k * tm:(k + 1) * tm, :]
    x2 = x1_ref[...] + g2_ref[0] * moe
    ms = jnp.mean(x2 * x2, axis=-1, keepdims=True)
    o_ref[...] = x2 * lax.rsqrt(ms + NORM_EPS) * fg_ref[...]


def _combine(pos, gate_t, x1, g2, fg, y_buf, *, tokens_per_batch, tm=256):
    n = x1.shape[0]
    bpb = tokens_per_batch // tm
    return pl.pallas_call(
        _combine_kernel,
        out_shape=jax.ShapeDtypeStruct((n, D_MODEL), F32),
        grid=(n // tm,),
        in_specs=[pl.BlockSpec((TOP_K, tm), lambda i: (0, i), memory_space=pltpu.SMEM),
                  pl.BlockSpec((tm, 8), lambda i: (i, 0)),
                  pl.BlockSpec((tm, D_MODEL), lambda i: (i, 0)),
                  pl.BlockSpec((1, 1, D_MODEL), lambda i: (i // bpb, 0, 0)),
                  pl.BlockSpec((1, D_MODEL), lambda i: (0, 0)),
                  pl.BlockSpec(memory_space=pl.ANY)],
        out_specs=pl.BlockSpec((tm, D_MODEL), lambda i: (i, 0)),
        scratch_shapes=[pltpu.VMEM((TOP_K * tm, D_MODEL), F32), pltpu.SemaphoreType.DMA(())],
        compiler_params=_cparams(("arbitrary",)),
        name="combine",
    )(pos, gate_t, x1, g2, fg, y_buf)


def _rope_tables(t):
    pos = jnp.arange(t, dtype=jnp.int32)
    row = (pos // GRID_W).astype(F32)
    col = (pos % GRID_W).astype(F32)
    n_freq = DH_D // 4
    inv_freq = ROPE_BASE ** (-jnp.arange(n_freq, dtype=F32) / n_freq)
    lane = jnp.arange(LANE)
    j = lane % n_freq
    use_col = (lane % DH_D) >= (DH_D // 2)
    ang = jnp.where(use_col[None, :], col[:, None] * inv_freq[j][None, :], row[:, None] * inv_freq[j][None, :])
    sign = jnp.where((lane % 32) < 16, -1.0, 1.0).astype(F32)
    return jnp.cos(ang), jnp.sin(ang) * sign[None, :]


def _permute_w_in(w):
    qm = w[:, 0:512] * (DQK_M ** -0.5)
    dq = w[:, 3088:4112] * (DH_D ** -0.5)
    main = jnp.concatenate([qm, w[:, 512:2048], w[:, 2064:3088], dq, w[:, 4112:8208]], axis=1)
    gates = jnp.pad(w[:, 2048:2064], ((0, 0), (0, NP_GATE - 4 * NH_M)))
    return jnp.concatenate([main, gates], axis=1).astype(BF16)


def kernel(x, c, ctx, c_ctx, ada_w, ada_b, norm1_g, norm2_g, w_in, mlstm_gate_b, mlstm_norm_g, diff_lambda,
           diff_norm_g, w_branch_a, w_branch_b, w_out, router_w, router_b, exp_w1, exp_b1, exp_w2, exp_b2,
           final_norm_g):
    assert ada_w.shape[0] == 1, "single-layer block"
    b, t, d = x.shape
    n_ctx = ctx.shape[1]
    n_tok = b * t

    rows = -(-(b + 1) // 8) * 8
    cvec = jnp.zeros((rows, d), F32).at[:b].set(c).at[b].set(c_ctx)
    mod = _ada(cvec, ada_w[0], ada_b[0])
    sh1, sc1, g1, sh2, sc2, g2 = [mod[:b, i * d:(i + 1) * d].reshape(b, 1, d) for i in range(6)]
    sh1c, sc1c = [mod[b:b + 1, i * d:(i + 1) * d].reshape(1, 1, d) for i in range(2)]

    w_all = _permute_w_in(w_in[0])
    cos_t, sin_t = _rope_tables(t)
    n1g = norm1_g[0].reshape(1, d)
    p_lat, gt_lat = _inproj(x.reshape(n_tok, d), sh1, sc1, n1g, w_all, cos_t, sin_t,
                            tokens_per_batch=t, tm=512, rope=True)
    p_ctx, gt_ctx = _inproj(ctx.reshape(b * n_ctx, d), sh1c, sc1c, n1g, w_all, cos_t[:n_ctx], sin_t[:n_ctx],
                            tokens_per_batch=n_ctx, tm=n_ctx, rope=False)
    pl3 = p_lat.reshape(b, t, NP_MAIN)
    pc3 = p_ctx.reshape(b, n_ctx, NP_MAIN)

    gates = jnp.concatenate([gt_ctx[:, :4 * NH_M].reshape(b, n_ctx, 4, NH_M),
                             gt_lat[:, :4 * NH_M].reshape(b, t, 4, NH_M)], axis=1)
    gates = gates.transpose(0, 3, 2, 1).reshape(b, NH_M, 4, (n_ctx + t) // CHUNK, CHUNK)
    a_lat = _mlstm(mlstm_gate_b[0], gates, pc3, pl3, mlstm_norm_g[0])

    b_lat = _diffattn(diff_lambda[0], diff_norm_g[0], pc3, pl3)

    x1, u2, idx, gate, gate_t, cnt = _merge(
        a_lat.reshape(n_tok, d), b_lat.reshape(n_tok, d), p_lat, x.reshape(n_tok, d), g1, sh2, sc2,
        norm2_g[0].reshape(1, d), w_branch_a[0].astype(BF16), w_branch_b[0].astype(BF16), w_out[0].astype(BF16),
        router_w[0].T, router_b[0].reshape(N_EXPERTS, 1), tokens_per_batch=t)
    del gate

    n_asg = n_tok * TOP_K
    n_blk = n_asg // EXPERT_BM + N_EXPERTS
    n_buf = n_blk * EXPERT_BM
    cnt_i = cnt[:, 0].astype(jnp.int32)
    padded = (cnt_i + EXPERT_BM - 1) // EXPERT_BM * EXPERT_BM
    pad_end = jnp.cumsum(padded)
    pad_start = pad_end - padded
    n_used = pad_end[-1] // EXPERT_BM
    blk = jnp.arange(n_blk, dtype=jnp.int32)
    blk_i = jnp.minimum(blk, n_used - 1)
    blk_e = jnp.minimum(jnp.searchsorted(pad_end, blk_i * EXPERT_BM, side='right'), N_EXPERTS - 1).astype(jnp.int32)
    last_blk = jnp.maximum(pad_end // EXPERT_BM - 1, 0).astype(jnp.int32)

    pos = _positions(idx, pad_start.astype(F32).reshape(N_EXPERTS, 1))
    n_used = n_used.reshape(1).astype(jnp.int32)
    xs = _dispatch(last_blk, n_used, pos, u2, n_buf)
    y_buf = _experts(blk_e, blk_i.astype(jnp.int32), n_used, xs,
                     exp_w1[0].astype(BF16), exp_b1[0].reshape(N_EXPERTS, 1, 2 * D_FF),
                     exp_w2[0].astype(BF16), exp_b2[0].reshape(N_EXPERTS, 1, D_MODEL))
    out = _combine(pos, gate_t, x1, g2, final_norm_g.reshape(1, d), y_buf, tokens_per_batch=t)
    return out.reshape(b, t, d)
```

```python
import functools
import math

import jax
import jax.numpy as jnp
from jax import lax
from jax.experimental import pallas as pl
from jax.experimental.pallas import tpu as pltpu

D_MODEL = 1024
GRID_W = 64
NH_M = 4
DQK_M = 128
DV_M = 256
CHUNK = 128
NH_D = 8
DH_D = 64
ROPE_BASE = 10000.0
N_EXPERTS = 32
TOP_K = 4
D_FF = 1024
SWIGLU_LIMIT = 7.0
SWIGLU_ALPHA = 1.702
NORM_EPS = 1e-6
LAM_INIT = 0.8 - 0.6 * math.exp(-0.3 * 0)

C_QM, C_KM, C_VM, C_OM = 0, 512, 1024, 2048
C_DQ, C_DK, C_DV, C_GA, C_GB = 3072, 4096, 5120, 6144, 7168
NP_MAIN = 8192
NP_GATE = 128
NP_ALL = NP_MAIN + NP_GATE

LANE = 128
EXPERT_BM = 512
MOE_TM = 512
SEG_ALIGN = 8
SEG_PIECES = (512, 256, 128, 64, 32, 16, 8)
TILE_ROWS = MOE_TM * TOP_K + N_EXPERTS * SEG_ALIGN
PERM_ROWS = 256
QK_ROWS = 128
PV_ROWS = 256
VMEM_LIMIT = 56 * 1024 * 1024

BF16 = jnp.bfloat16
F32 = jnp.float32


def _cparams(sem, flags=None):
    return pltpu.CompilerParams(dimension_semantics=sem, vmem_limit_bytes=VMEM_LIMIT, flags=flags)


def _ada_kernel(c_ref, w_ref, b_ref, o_ref):
    c = c_ref[...]
    s = c * jax.nn.sigmoid(c)
    o_ref[...] = jnp.dot(s, w_ref[...], preferred_element_type=F32,
                         precision=lax.Precision.HIGHEST) + b_ref[...]


def _ada(cvec, ada_w, ada_b):
    rows = cvec.shape[0]
    tn = 1024
    return pl.pallas_call(
        _ada_kernel,
        out_shape=jax.ShapeDtypeStruct((rows, 6 * D_MODEL), F32),
        grid=(6 * D_MODEL // tn,),
        in_specs=[pl.BlockSpec((rows, D_MODEL), lambda j: (0, 0)),
                  pl.BlockSpec((D_MODEL, tn), lambda j: (0, j)),
                  pl.BlockSpec((1, tn), lambda j: (0, j))],
        out_specs=pl.BlockSpec((rows, tn), lambda j: (0, j)),
        compiler_params=_cparams(("arbitrary",)),
        name="ada",
    )(cvec, ada_w, ada_b.reshape(1, -1))


def _inproj_kernel(x_ref, sh_ref, sc_ref, g_ref, w_ref, cos_ref, sin_ref, p_ref, gt_ref, *, rope, tn):
    x = x_ref[...]
    ms = jnp.mean(x * x, axis=-1, keepdims=True)
    u = (x * lax.rsqrt(ms + NORM_EPS) * g_ref[...]) * (1.0 + sc_ref[0]) + sh_ref[0]
    ub = u.astype(BF16)
    if rope:
        cos = cos_ref[...]
        sin = sin_ref[...]
        lane = lax.broadcasted_iota(jnp.int32, cos.shape, 1)
        lo = (lane % 32) < 16
    for c0 in range(0, NP_MAIN, tn):
        acc = jnp.dot(ub, w_ref[:, c0:c0 + tn], preferred_element_type=F32)
        if rope and C_DQ <= c0 < C_DV:
            for g0 in range(0, tn, LANE):
                xg = acc[:, g0:g0 + LANE]
                sw = jnp.where(lo, pltpu.roll(xg, LANE - 16, 1), pltpu.roll(xg, 16, 1))
                p_ref[:, c0 + g0:c0 + g0 + LANE] = (xg * cos + sw * sin).astype(BF16)
        else:
            p_ref[:, c0:c0 + tn] = acc.astype(BF16)
    gt_ref[...] = jnp.dot(ub, w_ref[:, NP_MAIN:], preferred_element_type=F32)


def _inproj(x2d, sh, sc, g, w_all, cos_t, sin_t, *, tokens_per_batch, tm, rope):
    n = x2d.shape[0]
    blocks_per_batch = tokens_per_batch // tm
    per_batch = sh.shape[0] > 1
    mod_map = (lambda i: (i // blocks_per_batch, 0, 0)) if per_batch else (lambda i: (0, 0, 0))
    tab_map = (lambda i: (i % blocks_per_batch, 0)) if rope else (lambda i: (0, 0))
    kern = functools.partial(_inproj_kernel, rope=rope, tn=512)
    return pl.pallas_call(
        kern,
        out_shape=(jax.ShapeDtypeStruct((n, NP_MAIN), BF16),
                   jax.ShapeDtypeStruct((n, NP_GATE), F32)),
        grid=(n // tm,),
        in_specs=[pl.BlockSpec((tm, D_MODEL), lambda i: (i, 0)),
                  pl.BlockSpec((1, 1, D_MODEL), mod_map),
                  pl.BlockSpec((1, 1, D_MODEL), mod_map),
                  pl.BlockSpec((1, D_MODEL), lambda i: (0, 0)),
                  pl.BlockSpec((D_MODEL, NP_ALL), lambda i: (0, 0), pipeline_mode=pl.Buffered(1)),
                  pl.BlockSpec((tm, LANE), tab_map),
                  pl.BlockSpec((tm, LANE), tab_map)],
        out_specs=(pl.BlockSpec((tm, NP_MAIN), lambda i: (i, 0)),
                   pl.BlockSpec((tm, NP_GATE), lambda i: (i, 0))),
        compiler_params=_cparams(("arbitrary",)),
        name="inproj_lat" if rope else "inproj_ctx",
    )(x2d, sh, sc, g, w_all, cos_t, sin_t)


def _log_sigmoid(x):
    return jnp.minimum(x, 0.0) - jnp.log1p(jnp.exp(-jnp.abs(x)))


def _cumsum_lanes(x):
    lane = lax.broadcasted_iota(jnp.int32, x.shape, 1)
    sh = 1
    while sh < x.shape[1]:
        x = x + jnp.where(lane >= sh, pltpu.roll(x, sh, 1), 0.0)
        sh *= 2
    return x


MLSTM_HP = 2
DVX = DV_M + LANE


def _mlstm_kernel(gb_ref, gates_ref, qc_ref, kc_ref, vc_ref, ql_ref, kl_ref, vl_ref, ol_ref, ng_ref,
                  out_ref, hsum_ref, c_ref, gs_ref, *, n_ctx_chunks, n_lat_chunks):
    L = CHUNK
    hp_idx = pl.program_id(1)
    nc = n_ctx_chunks + n_lat_chunks
    ones_blk = jnp.ones((L, LANE), BF16)

    for hh in range(MLSTM_HP):
        g = gates_ref[0, hh]
        h_idx = hp_idx * MLSTM_HP + hh
        for d in range(2):
            i_pre = g[2 * d] + gb_ref[(2 * d) * NH_M + h_idx]
            f_pre = g[2 * d + 1] + gb_ref[(2 * d + 1) * NH_M + h_idx]
            logf = _log_sigmoid(f_pre)
            pre = _cumsum_lanes(logf)
            tot = pre[:, L - 1:L]
            bc = pre if d == 0 else tot - pre + logf
            lw = tot - bc + i_pre
            gs_ref[hh, d, 0] = bc
            gs_ref[hh, d, 1] = i_pre - bc
            gs_ref[hh, d, 2] = lw
            gs_ref[hh, d, 3] = jnp.broadcast_to(tot, lw.shape)
            gs_ref[hh, d, 4] = jnp.broadcast_to(jnp.max(lw, axis=1, keepdims=True), lw.shape)

    c_ref[...] = jnp.zeros(c_ref.shape, F32)

    r_i = lax.broadcasted_iota(jnp.int32, (L, L), 0)
    c_i = lax.broadcasted_iota(jnp.int32, (L, L), 1)
    eye = r_i == c_i
    masks = (c_i <= r_i, c_i >= r_i)

    def to_col(row):
        return jnp.sum(jnp.where(eye, row, 0.0), axis=1, keepdims=True)

    def step(hh, d, gchunk, q, k, v, m, need_h):
        bc = gs_ref[hh, d, 0, pl.ds(gchunk, 1), :]
        cm = gs_ref[hh, d, 1, pl.ds(gchunk, 1), :]
        lw = gs_ref[hh, d, 2, pl.ds(gchunk, 1), :]
        bt = gs_ref[hh, d, 3, pl.ds(gchunk, 1), :][:, :1]
        ml = gs_ref[hh, d, 4, pl.ds(gchunk, 1), :][:, :1]
        vx = jnp.concatenate([v, ones_blk], axis=1)
        cmat = c_ref[hh, d]
        h = None
        if need_h:
            a_col = to_col(bc)
            log_d = jnp.where(masks[d], a_col + cm, -jnp.inf)
            inter = a_col + m
            m_t = jnp.maximum(inter, jnp.max(log_d, axis=1, keepdims=True))
            dmat = jnp.exp(log_d - m_t)
            s = lax.dot_general(q, k, (((1,), (1,)), ((), ())), preferred_element_type=F32) * dmat
            w_inter = jnp.exp(inter - m_t)
            qcn = jnp.dot(q, cmat.astype(BF16), preferred_element_type=F32)
            svr = jnp.dot(s.astype(BF16), vx, preferred_element_type=F32)
            num = w_inter * qcn[:, :DV_M] + svr[:, :DV_M]
            den = w_inter * qcn[:, DV_M:DV_M + 1] + svr[:, DV_M:DV_M + 1]
            h = num / jnp.maximum(jnp.abs(den), jnp.exp(-m_t))
        m_new = jnp.maximum(bt + m, ml)
        w_col = to_col(jnp.exp(lw - m_new))
        decay = jnp.exp(bt + m - m_new)
        kw = (k.astype(F32) * w_col).astype(BF16)
        c_ref[hh, d] = decay * cmat + lax.dot_general(kw, vx, (((0,), (0,)), ((), ())),
                                                      preferred_element_type=F32)
        return h, m_new

    def head_cols(ref, rows, hh, width):
        return ref[0, rows, hh * width:(hh + 1) * width]

    ms_state = [jnp.zeros((1, 1), F32) for _ in range(2 * MLSTM_HP)]

    for j in range(n_ctx_chunks):
        for hh in range(MLSTM_HP):
            for d, jc in ((0, j), (1, n_ctx_chunks - 1 - j)):
                rows = slice(jc * L, (jc + 1) * L)
                _, ms_state[2 * hh + d] = step(hh, d, jc, head_cols(qc_ref, rows, hh, DQK_M),
                                               head_cols(kc_ref, rows, hh, DQK_M),
                                               head_cols(vc_ref, rows, hh, DV_M), ms_state[2 * hh + d], False)

    def finish(rows, hh, hs):
        ms = jnp.mean(hs * hs, axis=-1, keepdims=True)
        hn = hs * lax.rsqrt(ms + NORM_EPS) * ng_ref[hh]
        o = jax.nn.sigmoid(head_cols(ol_ref, rows, hh, DV_M).astype(F32))
        out_ref[0, rows, hh * DV_M:(hh + 1) * DV_M] = (hn * o).astype(out_ref.dtype)

    def lat_pair(j, carry, second_half):
        carry = list(carry)
        jb = n_lat_chunks - 1 - j
        for hh in range(MLSTM_HP):
            for d, jc in ((0, j), (1, jb)):
                rows = pl.ds(pl.multiple_of(jc * L, L), L)
                h, carry[2 * hh + d] = step(hh, d, n_ctx_chunks + jc, head_cols(ql_ref, rows, hh, DQK_M),
                                            head_cols(kl_ref, rows, hh, DQK_M),
                                            head_cols(vl_ref, rows, hh, DV_M), carry[2 * hh + d], True)
                if second_half:
                    finish(rows, hh, hsum_ref[hh, rows, :] + h)
                else:
                    hsum_ref[hh, rows, :] = h
        return tuple(carry)

    half = n_lat_chunks // 2
    carry = lax.fori_loop(0, half, lambda j, c: lat_pair(j, c, False), tuple(ms_state))
    lax.fori_loop(half, n_lat_chunks, lambda j, c: lat_pair(j, c, True), carry)


def _mlstm(gate_b, gates_all, pc3, pl3, norm_g):
    b, t = pl3.shape[0], pl3.shape[1]
    ctx = pc3.shape[1]
    n_ctx_chunks, n_lat_chunks = ctx // CHUNK, t // CHUNK
    assert n_lat_chunks % 2 == 0 and NH_M % MLSTM_HP == 0
    nc = n_ctx_chunks + n_lat_chunks
    hp = MLSTM_HP
    kern = functools.partial(_mlstm_kernel, n_ctx_chunks=n_ctx_chunks, n_lat_chunks=n_lat_chunks)
    qw, vw = hp * DQK_M, hp * DV_M
    qo, ko, vo, oo = C_QM // qw, C_KM // qw, C_VM // vw, C_OM // vw
    return pl.pallas_call(
        kern,
        out_shape=jax.ShapeDtypeStruct((b, t, NH_M * DV_M), BF16),
        grid=(b, NH_M // hp),
        in_specs=[pl.BlockSpec(memory_space=pltpu.SMEM),
                  pl.BlockSpec((1, hp, 4, nc, CHUNK), lambda i, h: (i, h, 0, 0, 0)),
                  pl.BlockSpec((1, ctx, qw), lambda i, h: (i, 0, qo + h)),
                  pl.BlockSpec((1, ctx, qw), lambda i, h: (i, 0, ko + h)),
                  pl.BlockSpec((1, ctx, vw), lambda i, h: (i, 0, vo + h)),
                  pl.BlockSpec((1, t, qw), lambda i, h: (i, 0, qo + h)),
                  pl.BlockSpec((1, t, qw), lambda i, h: (i, 0, ko + h)),
                  pl.BlockSpec((1, t, vw), lambda i, h: (i, 0, vo + h)),
                  pl.BlockSpec((1, t, vw), lambda i, h: (i, 0, oo + h)),
                  pl.BlockSpec((hp, 1, DV_M), lambda i, h: (h, 0, 0))],
        out_specs=pl.BlockSpec((1, t, vw), lambda i, h: (i, 0, h)),
        scratch_shapes=[pltpu.VMEM((hp, t, DV_M), F32),
                        pltpu.VMEM((hp, 2, DQK_M, DVX), F32),
                        pltpu.VMEM((hp, 2, 5, nc, CHUNK), F32)],
        compiler_params=_cparams(("arbitrary", "arbitrary")),
        name="mlstm",
    )(gate_b, gates_all, pc3, pc3, pc3, pl3, pl3, pl3, pl3, norm_g.reshape(NH_M, 1, DV_M))


def _split_rows(n, size):
    return [(c, min(size, n - c)) for c in range(0, n, size)]


def _attn_kernel(lam_ref, g_ref, q_ref, kc_ref, kl_ref, vc_ref, vl_ref, o_ref,
                 kcat_ref, vt_ref, sa_ref, sb_ref, mxa_ref, mxb_ref, acc_ref, *, ctx, t, tq, n_iter):
    nk = ctx + t
    rows_it = nk // n_iter
    nq = t // tq

    kcat_ref[0:ctx, :] = kc_ref[0]
    kcat_ref[ctx:nk, :] = kl_ref[0]
    for (v_ref, s0, n) in ((vc_ref, 0, ctx), (vl_ref, ctx, t)):
        for (c, sz) in _split_rows(n, 512):
            vt_ref[:, s0 + c:s0 + c + sz] = v_ref[0, c:c + sz, :].astype(F32).T.astype(BF16)

    def q_rows(j):
        start = j * tq
        return pl.ds(start if isinstance(start, int) else pl.multiple_of(start, tq), tq)

    def step(j, s_cur, s_prv, mx_cur, mx_prv, scores, softmax):
        if scores:
            qt = q_ref[0, q_rows(j), :].astype(F32).T
            row = lax.broadcasted_iota(jnp.int32, qt.shape, 0)
            qtm = (jnp.where(row < DH_D, qt, 0.0).astype(BF16), jnp.where(row >= DH_D, qt, 0.0).astype(BF16))
        if softmax:
            mprev = (mx_prv[0:1, :], mx_prv[1:2, :])
            acc_ref[...] = jnp.zeros(acc_ref.shape, F32)

        def body(i, carry):
            mrun, lrun = carry
            base = pl.multiple_of(i * rows_it, LANE)
            new_m, new_l = [], []
            for m in range(2):
                mm, ll = mrun[m], lrun[m]
                if scores:
                    for (c, sz) in _split_rows(rows_it, QK_ROWS):
                        rows = pl.ds(base + c, sz)
                        s = jnp.dot(kcat_ref[rows, :], qtm[m], preferred_element_type=F32)
                        s_cur[m, rows, :] = s
                        mm = jnp.maximum(mm, jnp.max(s.reshape(sz // 8, 8, tq), axis=0))
                if softmax:
                    acc = acc_ref[m]
                    for (c, sz) in _split_rows(rows_it, PV_ROWS):
                        rows = pl.ds(base + c, sz)
                        p = jnp.exp2(s_prv[m, rows, :] - mprev[m])
                        ll = ll + jnp.sum(p.reshape(sz // 8, 8, tq), axis=0)
                        acc = acc + jnp.dot(vt_ref[:, rows], p.astype(BF16), preferred_element_type=F32)
                    acc_ref[m] = acc
                new_m.append(mm)
                new_l.append(ll)
            return jnp.stack(new_m), jnp.stack(new_l)

        init = (jnp.full((2, 8, tq), -jnp.inf, F32), jnp.zeros((2, 8, tq), F32))
        mrun, lrun = lax.fori_loop(0, n_iter, body, init)
        if scores:
            for m in range(2):
                mx_cur[m:m + 1, :] = jnp.max(mrun[m], axis=0, keepdims=True)
        if softmax:
            lp = lam_ref[...]
            lam = (jnp.exp(jnp.sum(lp[0:1] * lp[1:2], axis=1, keepdims=True))
                   - jnp.exp(jnp.sum(lp[2:3] * lp[3:4], axis=1, keepdims=True)) + LAM_INIT)
            l0 = jnp.sum(lrun[0], axis=0, keepdims=True)
            l1 = jnp.sum(lrun[1], axis=0, keepdims=True)
            o = acc_ref[0] * (1.0 / l0) - acc_ref[1] * (lam / l1)
            ms = jnp.mean(o * o, axis=0, keepdims=True)
            on = o * lax.rsqrt(ms + NORM_EPS) * (g_ref[0] * (1.0 - LAM_INIT))
            o_ref[0, q_rows(j - 1), :] = on.T.astype(o_ref.dtype)

    bufs = ((sa_ref, sb_ref, mxa_ref, mxb_ref), (sb_ref, sa_ref, mxb_ref, mxa_ref))
    step(0, *bufs[0], True, False)

    def pair(i, c):
        step(2 * i + 1, *bufs[1], True, True)
        step(2 * i + 2, *bufs[0], True, True)
        return c

    n_pairs = (nq - 1) // 2
    lax.fori_loop(0, n_pairs, pair, 0)
    for j in range(2 * n_pairs + 1, nq):
        step(j, *bufs[j % 2], True, True)
    step(nq, *bufs[nq % 2], False, True)


def _diffattn(lam_p, norm_g, pc3, pl3, *, tq=256, n_iter=1):
    b, t = pl3.shape[0], pl3.shape[1]
    ctx = pc3.shape[1]
    nk = ctx + t
    assert nk % (n_iter * LANE) == 0
    kern = functools.partial(_attn_kernel, ctx=ctx, t=t, tq=tq, n_iter=n_iter)
    qo, ko, vo = C_DQ // LANE, C_DK // LANE, C_DV // LANE
    return pl.pallas_call(
        kern,
        out_shape=jax.ShapeDtypeStruct((b, t, NH_D * 2 * DH_D), BF16),
        grid=(b, NH_D),
        in_specs=[pl.BlockSpec((4, DH_D), lambda i, h: (0, 0)),
                  pl.BlockSpec((1, 2 * DH_D, 1), lambda i, h: (h, 0, 0)),
                  pl.BlockSpec((1, t, LANE), lambda i, h: (i, 0, qo + h)),
                  pl.BlockSpec((1, ctx, LANE), lambda i, h: (i, 0, ko + h)),
                  pl.BlockSpec((1, t, LANE), lambda i, h: (i, 0, ko + h)),
                  pl.BlockSpec((1, ctx, LANE), lambda i, h: (i, 0, vo + h)),
                  pl.BlockSpec((1, t, LANE), lambda i, h: (i, 0, vo + h))],
        out_specs=pl.BlockSpec((1, t, LANE), lambda i, h: (i, 0, h)),
        scratch_shapes=[pltpu.VMEM((nk, LANE), BF16), pltpu.VMEM((2 * DH_D, nk), BF16),
                        pltpu.VMEM((2, nk, tq), F32), pltpu.VMEM((2, nk, tq), F32),
                        pltpu.VMEM((8, tq), F32), pltpu.VMEM((8, tq), F32),
                        pltpu.VMEM((2, 2 * DH_D, tq), F32)],
        compiler_params=_cparams(("arbitrary", "arbitrary")),
        name="diffattn",
    )(lam_p, norm_g.reshape(NH_D, 2 * DH_D, 1), pl3, pc3, pl3, pc3, pl3)


def _merge_kernel(a_ref, b_ref, ga_ref, gb_ref, x_ref, g1_ref, sh2_ref, sc2_ref, n2_ref,
                  wa_ref, wb_ref, wo_ref, rw_ref, rb_ref,
                  x1_ref, u2_ref, idx_ref, gate_ref, cnt_ref):
    ya = jnp.dot(a_ref[...], wa_ref[...], preferred_element_type=F32)
    yb = jnp.dot(b_ref[...], wb_ref[...], preferred_element_type=F32)
    y = jax.nn.sigmoid(ga_ref[...].astype(F32)) * ya + jax.nn.sigmoid(gb_ref[...].astype(F32)) * yb
    y2 = jnp.dot(y.astype(BF16), wo_ref[...], preferred_element_type=F32)
    x1 = x_ref[...] + g1_ref[0] * y2
    x1_ref[...] = x1
    ms = jnp.mean(x1 * x1, axis=-1, keepdims=True)
    u2 = (x1 * lax.rsqrt(ms + NORM_EPS) * n2_ref[...]) * (1.0 + sc2_ref[0]) + sh2_ref[0]
    u2_ref[...] = u2.astype(u2_ref.dtype)
    logits = lax.dot_general(rw_ref[...], u2, (((1,), (1,)), ((), ())), preferred_element_type=F32,
                             precision=lax.Precision.HIGHEST) + rb_ref[...]
    tm = logits.shape[1]
    e_i = lax.broadcasted_iota(jnp.int32, logits.shape, 0)
    vals, cnt = [], jnp.zeros((N_EXPERTS, 1), F32)
    work = logits
    for k in range(TOP_K):
        mx = jnp.max(work, axis=0, keepdims=True)
        sel = jnp.min(jnp.where(work == mx, e_i, N_EXPERTS), axis=0, keepdims=True)
        hit = e_i == sel
        work = jnp.where(hit, -jnp.inf, work)
        idx_ref[k:k + 1, :] = sel
        vals.append(mx)
        cnt = cnt + jnp.sum(hit.astype(F32), axis=1, keepdims=True)
    ex = [jnp.exp(v - vals[0]) for v in vals]
    inv = 1.0 / (ex[0] + ex[1] + ex[2] + ex[3])
    for k in range(TOP_K):
        gate_ref[k:k + 1, :] = ex[k] * inv
    cnt_ref[0] = cnt


def _merge(a_lat, b_lat, p_lat, x2d, g1, sh2, sc2, n2g, wa, wb, wo, rwt, rb, *, tokens_per_batch, tm=MOE_TM):
    n = x2d.shape[0]
    bpb = tokens_per_batch // tm
    mod_map = lambda i: (i // bpb, 0, 0)
    row = lambda i: (i, 0)
    const = lambda i: (0, 0)
    return pl.pallas_call(
        _merge_kernel,
        out_shape=(jax.ShapeDtypeStruct((n, D_MODEL), F32),
                   jax.ShapeDtypeStruct((n, D_MODEL), BF16),
                   jax.ShapeDtypeStruct((TOP_K, n), jnp.int32),
                   jax.ShapeDtypeStruct((TOP_K, n), F32),
                   jax.ShapeDtypeStruct((n // tm, N_EXPERTS, 1), F32)),
        grid=(n // tm,),
        in_specs=[pl.BlockSpec((tm, D_MODEL), row),
                  pl.BlockSpec((tm, D_MODEL), row),
                  pl.BlockSpec((tm, D_MODEL), lambda i: (i, C_GA // D_MODEL)),
                  pl.BlockSpec((tm, D_MODEL), lambda i: (i, C_GB // D_MODEL)),
                  pl.BlockSpec((tm, D_MODEL), row),
                  pl.BlockSpec((1, 1, D_MODEL), mod_map),
                  pl.BlockSpec((1, 1, D_MODEL), mod_map),
                  pl.BlockSpec((1, 1, D_MODEL), mod_map),
                  pl.BlockSpec((1, D_MODEL), const),
                  pl.BlockSpec((D_MODEL, D_MODEL), const),
                  pl.BlockSpec((D_MODEL, D_MODEL), const),
                  pl.BlockSpec((D_MODEL, D_MODEL), const),
                  pl.BlockSpec((N_EXPERTS, D_MODEL), const),
                  pl.BlockSpec((N_EXPERTS, 1), const)],
        out_specs=(pl.BlockSpec((tm, D_MODEL), row),
                   pl.BlockSpec((tm, D_MODEL), row),
                   pl.BlockSpec((TOP_K, tm), lambda i: (0, i)),
                   pl.BlockSpec((TOP_K, tm), lambda i: (0, i)),
                   pl.BlockSpec((1, N_EXPERTS, 1), lambda i: (i, 0, 0))),
        compiler_params=_cparams(("arbitrary",)),
        name="merge",
    )(a_lat, b_lat, p_lat, p_lat, x2d, g1, sh2, sc2, n2g, wa, wb, wo, rwt, rb)


def _segment_copies(tile, seg_ref, len_ref, off_ref, make_copy):
    def seg(e, rows):
        s = tile * N_EXPERTS + e
        o, d, n = off_ref[s], seg_ref[s], len_ref[s]
        for p in SEG_PIECES:
            take = n & p

            @pl.when(take != 0)
            def _():
                make_copy(pl.multiple_of(o, SEG_ALIGN), pl.multiple_of(d, SEG_ALIGN), p).start()
            o = o + take
            d = d + take
        return rows + n
    return lax.fori_loop(0, N_EXPERTS, seg, 0)


def _wait_rows(rows, make_copy):
    big = SEG_PIECES[3]

    def wait_big(j, c):
        make_copy(big).wait()
        return c

    def wait_small(j, c):
        make_copy(SEG_ALIGN).wait()
        return c

    lax.fori_loop(0, rows // big, wait_big, 0)
    lax.fori_loop(0, (rows % big) // SEG_ALIGN, wait_small, 0)


def _dispatch_kernel(seg_ref, len_ref, off_ref, trows_ref, last_ref, nu_ref, idx_ref, offc_ref, u_ref,
                     lr_ref, xs_ref, sorted_ref, zbuf, zsem, sem, *, n_blk):
    i = pl.program_id(0)
    tm = u_ref.shape[0]
    slot = lax.rem(i, 2)

    def zero_block(j):
        return pltpu.make_async_copy(zbuf, xs_ref.at[pl.ds(pl.multiple_of(j * EXPERT_BM, EXPERT_BM), EXPERT_BM)],
                                     zsem)

    @pl.when(i == 0)
    def _():
        zbuf[...] = jnp.zeros(zbuf.shape, zbuf.dtype)
        n_used = nu_ref[0]

        def start_last(e, c):
            zero_block(last_ref[e]).start()
            return c

        def start_tail(j, c):
            zero_block(j).start()
            return c

        def wait_one(j, c):
            zero_block(0).wait()
            return c

        lax.fori_loop(0, N_EXPERTS, start_last, 0)
        lax.fori_loop(n_used, n_blk, start_tail, 0)
        lax.fori_loop(0, N_EXPERTS + n_blk - n_used, wait_one, 0)

    upper = (lax.broadcasted_iota(jnp.int32, (tm, tm), 0)
             < lax.broadcasted_iota(jnp.int32, (tm, tm), 1)).astype(BF16)
    e_i = lax.broadcasted_iota(jnp.int32, (N_EXPERTS, tm), 0)
    run = offc_ref[0]
    ranks = []
    for k in range(TOP_K):
        hit = e_i == idx_ref[k:k + 1, :]
        onehot = hit.astype(F32)
        before = jnp.dot(onehot.astype(BF16), upper, preferred_element_type=F32)
        rk = jnp.sum(jnp.where(hit, run + before, 0.0), axis=0, keepdims=True).astype(jnp.int32)
        lr_ref[k:k + 1, :] = rk
        ranks.append(rk)
        run = run + jnp.sum(onehot, axis=1, keepdims=True)

    u = u_ref[...]
    for r0 in range(0, TILE_ROWS, PERM_ROWS):
        r_i = lax.broadcasted_iota(jnp.int32, (PERM_ROWS, tm), 0) + r0
        perm = jnp.zeros((PERM_ROWS, tm), F32)
        for k in range(TOP_K):
            perm = jnp.where(r_i == ranks[k], 1.0, perm)
        sorted_ref[slot, r0:r0 + PERM_ROWS, :] = jnp.dot(perm.astype(BF16), u, preferred_element_type=F32)

    def wait_tile(tile):
        _wait_rows(trows_ref[tile],
                   lambda p: pltpu.make_async_copy(sorted_ref.at[0, pl.ds(0, p)], xs_ref.at[pl.ds(0, p)], sem))

    @pl.when(i > 0)
    def _():
        wait_tile(i - 1)

    _segment_copies(i, seg_ref, len_ref, off_ref,
                    lambda o, d, p: pltpu.make_async_copy(sorted_ref.at[slot, pl.ds(o, p)], xs_ref.at[pl.ds(d, p)],
                                                          sem))

    @pl.when(i == pl.num_programs(0) - 1)
    def _():
        wait_tile(i)


def _dispatch(seg, seglen, off, trows, last_blk, n_used, idx, offc, u2, n_buf, *, tm=MOE_TM):
    n = u2.shape[0]
    kern = functools.partial(_dispatch_kernel, n_blk=n_buf // EXPERT_BM)
    tile = lambda i, *_: (0, i)
    return pl.pallas_call(
        kern,
        out_shape=(jax.ShapeDtypeStruct((TOP_K, n), jnp.int32),
                   jax.ShapeDtypeStruct((n_buf, D_MODEL), F32)),
        grid_spec=pltpu.PrefetchScalarGridSpec(
            num_scalar_prefetch=6, grid=(n // tm,),
            in_specs=[pl.BlockSpec((TOP_K, tm), tile),
                      pl.BlockSpec((1, N_EXPERTS, 1), lambda i, *_: (i, 0, 0)),
                      pl.BlockSpec((tm, D_MODEL), lambda i, *_: (i, 0))],
            out_specs=(pl.BlockSpec((TOP_K, tm), tile),
                       pl.BlockSpec(memory_space=pl.ANY)),
            scratch_shapes=[pltpu.VMEM((2, TILE_ROWS, D_MODEL), F32),
                            pltpu.VMEM((EXPERT_BM, D_MODEL), F32),
                            pltpu.SemaphoreType.DMA(()), pltpu.SemaphoreType.DMA(())]),
        compiler_params=_cparams(("arbitrary",)),
        name="dispatch",
    )(seg, seglen, off, trows, last_blk, n_used, idx, offc, u2)


def _expert_kernel(be_ref, bi_ref, nu_ref, x_ref, w1_ref, b1_ref, w2_ref, b2_ref, y_ref, w1b_ref, w2b_ref):
    j = pl.program_id(0)

    @pl.when((j == 0) | (be_ref[j] != be_ref[jnp.maximum(j - 1, 0)]))
    def _():
        for c in range(0, 2 * D_FF, 512):
            w1b_ref[:, c:c + 512] = w1_ref[0, :, c:c + 512].astype(BF16)
        for c in range(0, D_MODEL, 512):
            w2b_ref[:, c:c + 512] = w2_ref[0, :, c:c + 512].astype(BF16)

    @pl.when(j < nu_ref[0])
    def _():
        hcat = jnp.dot(x_ref[...].astype(BF16), w1b_ref[...], preferred_element_type=F32) + b1_ref[0]
        h_glu = jnp.minimum(hcat[:, :D_FF], SWIGLU_LIMIT)
        h_lin = jnp.clip(hcat[:, D_FF:], -SWIGLU_LIMIT, SWIGLU_LIMIT)
        act = h_glu * jax.nn.sigmoid(SWIGLU_ALPHA * h_glu) * (h_lin + 1.0)
        y_ref[...] = jnp.dot(act.astype(BF16), w2b_ref[...], preferred_element_type=F32) + b2_ref[0]

    @pl.when(pl.program_id(0) >= nu_ref[0])
    def _():
        y_ref[...] = jnp.zeros(y_ref.shape, y_ref.dtype)


def _experts(blk_e, blk_i, n_used, xs, w1, b1, w2, b2):
    n_buf = xs.shape[0]
    n_blk = n_buf // EXPERT_BM
    return pl.pallas_call(
        _expert_kernel,
        out_shape=jax.ShapeDtypeStruct((n_buf, D_MODEL), F32),
        grid_spec=pltpu.PrefetchScalarGridSpec(
            num_scalar_prefetch=3, grid=(n_blk,),
            in_specs=[pl.BlockSpec((EXPERT_BM, D_MODEL), lambda j, be, bi, nu: (bi[j], 0)),
                      pl.BlockSpec((1, D_MODEL, 2 * D_FF), lambda j, be, bi, nu: (be[j], 0, 0)),
                      pl.BlockSpec((1, 1, 2 * D_FF), lambda j, be, bi, nu: (be[j], 0, 0)),
                      pl.BlockSpec((1, D_FF, D_MODEL), lambda j, be, bi, nu: (be[j], 0, 0)),
                      pl.BlockSpec((1, 1, D_MODEL), lambda j, be, bi, nu: (be[j], 0, 0))],
            out_specs=pl.BlockSpec((EXPERT_BM, D_MODEL), lambda j, be, bi, nu: (j, 0)),
            scratch_shapes=[pltpu.VMEM((D_MODEL, 2 * D_FF), BF16), pltpu.VMEM((D_FF, D_MODEL), BF16)]),
        compiler_params=_cparams(("arbitrary",)),
        name="experts",
    )(blk_e, blk_i, n_used, xs, w1, b1, w2, b2)


def _combine_kernel(seg_ref, len_ref, off_ref, trows_ref, lr_ref, gate_ref, x1_ref, g2_ref, fg_ref, y_ref,
                    o_ref, ybuf, acc_ref, sem):
    i = pl.program_id(0)
    tm = x1_ref.shape[0]
    slot = lax.rem(i, 2)

    def gather(tile, s):
        _segment_copies(tile, seg_ref, len_ref, off_ref,
                        lambda o, d, p: pltpu.make_async_copy(y_ref.at[pl.ds(d, p)], ybuf.at[s, pl.ds(o, p)],
                                                              sem.at[s]))

    @pl.when(i == 0)
    def _():
        ybuf[...] = jnp.zeros(ybuf.shape, F32)
        gather(0, 0)

    @pl.when(i + 1 < pl.num_programs(0))
    def _():
        gather(i + 1, 1 - slot)

    _wait_rows(trows_ref[i],
               lambda p: pltpu.make_async_copy(y_ref.at[pl.ds(0, p)], ybuf.at[slot, pl.ds(0, p)], sem.at[slot]))

    acc_ref[...] = jnp.zeros(acc_ref.shape, F32)
    for r0 in range(0, TILE_ROWS, PERM_ROWS):
        r_i = lax.broadcasted_iota(jnp.int32, (PERM_ROWS, tm), 0) + r0
        w = jnp.zeros((PERM_ROWS, tm), F32)
        for k in range(TOP_K):
            w = jnp.where(r_i == lr_ref[k:k + 1, :], gate_ref[k:k + 1, :], w)
        yb = ybuf[slot, r0:r0 + PERM_ROWS, :].astype(BF16)
        acc_ref[...] += lax.dot_general(w.astype(BF16), yb, (((0,), (0,)), ((), ())),
                                        preferred_element_type=F32)
    x2 = x1_ref[...] + g2_ref[0] * acc_ref[...]
    ms = jnp.mean(x2 * x2, axis=-1, keepdims=True)
    o_ref[...] = x2 * lax.rsqrt(ms + NORM_EPS) * fg_ref[...]


def _combine(seg, seglen, off, trows, lr, gate, x1, g2, fg, y_buf, *, tokens_per_batch, tm=MOE_TM):
    n = x1.shape[0]
    bpb = tokens_per_batch // tm
    return pl.pallas_call(
        _combine_kernel,
        out_shape=jax.ShapeDtypeStruct((n, D_MODEL), F32),
        grid_spec=pltpu.PrefetchScalarGridSpec(
            num_scalar_prefetch=4, grid=(n // tm,),
            in_specs=[pl.BlockSpec((TOP_K, tm), lambda i, *_: (0, i)),
                      pl.BlockSpec((TOP_K, tm), lambda i, *_: (0, i)),
                      pl.BlockSpec((tm, D_MODEL), lambda i, *_: (i, 0)),
                      pl.BlockSpec((1, 1, D_MODEL), lambda i, *_: (i // bpb, 0, 0)),
                      pl.BlockSpec((1, D_MODEL), lambda i, *_: (0, 0)),
                      pl.BlockSpec(memory_space=pl.ANY)],
            out_specs=pl.BlockSpec((tm, D_MODEL), lambda i, *_: (i, 0)),
            scratch_shapes=[pltpu.VMEM((2, TILE_ROWS, D_MODEL), F32), pltpu.VMEM((tm, D_MODEL), F32),
                            pltpu.SemaphoreType.DMA((2,))]),
        compiler_params=_cparams(("arbitrary",)),
        name="combine",
    )(seg, seglen, off, trows, lr, gate, x1, g2, fg, y_buf)


def _rope_tables(t):
    pos = jnp.arange(t, dtype=jnp.int32)
    row = (pos // GRID_W).astype(F32)
    col = (pos % GRID_W).astype(F32)
    n_freq = DH_D // 4
    inv_freq = ROPE_BASE ** (-jnp.arange(n_freq, dtype=F32) / n_freq)
    lane = jnp.arange(LANE)
    j = lane % n_freq
    use_col = (lane % DH_D) >= (DH_D // 2)
    ang = jnp.where(use_col[None, :], col[:, None] * inv_freq[j][None, :], row[:, None] * inv_freq[j][None, :])
    sign = jnp.where((lane % 32) < 16, -1.0, 1.0).astype(F32)
    return jnp.cos(ang), jnp.sin(ang) * sign[None, :]


def _permute_w_in(w):
    qm = w[:, 0:512] * (DQK_M ** -0.5)
    dq = w[:, 3088:4112] * (DH_D ** -0.5 * math.log2(math.e))
    main = jnp.concatenate([qm, w[:, 512:2048], w[:, 2064:3088], dq, w[:, 4112:8208]], axis=1)
    gates = jnp.pad(w[:, 2048:2064], ((0, 0), (0, NP_GATE - 4 * NH_M)))
    return jnp.concatenate([main, gates], axis=1).astype(BF16)


def kernel(x, c, ctx, c_ctx, ada_w, ada_b, norm1_g, norm2_g, w_in, mlstm_gate_b, mlstm_norm_g, diff_lambda,
           diff_norm_g, w_branch_a, w_branch_b, w_out, router_w, router_b, exp_w1, exp_b1, exp_w2, exp_b2,
           final_norm_g):
    assert ada_w.shape[0] == 1, "single-layer block"
    b, t, d = x.shape
    n_ctx = ctx.shape[1]
    n_tok = b * t

    rows = -(-(b + 1) // 8) * 8
    cvec = jnp.zeros((rows, d), F32).at[:b].set(c).at[b].set(c_ctx)
    mod = _ada(cvec, ada_w[0], ada_b[0])
    sh1, sc1, g1, sh2, sc2, g2 = [mod[:b, i * d:(i + 1) * d].reshape(b, 1, d) for i in range(6)]
    sh1c, sc1c = [mod[b:b + 1, i * d:(i + 1) * d].reshape(1, 1, d) for i in range(2)]

    w_all = _permute_w_in(w_in[0])
    cos_t, sin_t = _rope_tables(t)
    n1g = norm1_g[0].reshape(1, d)
    p_lat, gt_lat = _inproj(x.reshape(n_tok, d), sh1, sc1, n1g, w_all, cos_t, sin_t,
                            tokens_per_batch=t, tm=512, rope=True)
    p_ctx, gt_ctx = _inproj(ctx.reshape(b * n_ctx, d), sh1c, sc1c, n1g, w_all, cos_t[:n_ctx], sin_t[:n_ctx],
                            tokens_per_batch=n_ctx, tm=n_ctx, rope=False)
    pl3 = p_lat.reshape(b, t, NP_MAIN)
    pc3 = p_ctx.reshape(b, n_ctx, NP_MAIN)

    gates = jnp.concatenate([gt_ctx[:, :4 * NH_M].reshape(b, n_ctx, 4, NH_M),
                             gt_lat[:, :4 * NH_M].reshape(b, t, 4, NH_M)], axis=1)
    gates = gates.transpose(0, 3, 2, 1).reshape(b, NH_M, 4, (n_ctx + t) // CHUNK, CHUNK)
    a_lat = _mlstm(mlstm_gate_b[0], gates, pc3, pl3, mlstm_norm_g[0])

    b_lat = _diffattn(diff_lambda[0], diff_norm_g[0], pc3, pl3)

    x1, u2, idx, gate, cnt = _merge(
        a_lat.reshape(n_tok, d), b_lat.reshape(n_tok, d), p_lat, x.reshape(n_tok, d), g1, sh2, sc2,
        norm2_g[0].reshape(1, d), w_branch_a[0].astype(BF16), w_branch_b[0].astype(BF16), w_out[0].astype(BF16),
        router_w[0].T, router_b[0].reshape(N_EXPERTS, 1), tokens_per_batch=t)

    n_tiles = n_tok // MOE_TM
    n_blk = -(-(n_tok * TOP_K + n_tiles * N_EXPERTS * SEG_ALIGN) // EXPERT_BM) + N_EXPERTS
    n_buf = n_blk * EXPERT_BM
    seglen = (cnt[:, :, 0].astype(jnp.int32) + SEG_ALIGN - 1) // SEG_ALIGN * SEG_ALIGN
    padded = (jnp.sum(seglen, axis=0) + EXPERT_BM - 1) // EXPERT_BM * EXPERT_BM
    pad_end = jnp.cumsum(padded)
    pad_start = pad_end - padded
    seg = pad_start[None, :] + jnp.cumsum(seglen, axis=0) - seglen
    off = jnp.cumsum(seglen, axis=1) - seglen
    n_used = pad_end[-1] // EXPERT_BM
    blk = jnp.arange(n_blk, dtype=jnp.int32)
    blk_i = jnp.minimum(blk, n_used - 1)
    blk_e = jnp.sum((pad_end[None, :] <= (blk_i * EXPERT_BM)[:, None]).astype(jnp.int32), axis=1)
    blk_e = jnp.minimum(blk_e, N_EXPERTS - 1)
    last_blk = jnp.maximum(pad_end // EXPERT_BM - 1, 0).astype(jnp.int32)
    n_used = n_used.reshape(1).astype(jnp.int32)
    trows = jnp.sum(seglen, axis=1).astype(jnp.int32)
    seg, seglen, off = [a.reshape(-1).astype(jnp.int32) for a in (seg, seglen, off)]
    offc = off.astype(F32).reshape(n_tiles, N_EXPERTS, 1)

    lr, xs = _dispatch(seg, seglen, off, trows, last_blk, n_used, idx, offc, u2, n_buf)
    y_buf = _experts(blk_e, blk_i.astype(jnp.int32), n_used, xs,
                     exp_w1[0], exp_b1[0].reshape(N_EXPERTS, 1, 2 * D_FF),
                     exp_w2[0], exp_b2[0].reshape(N_EXPERTS, 1, D_MODEL))
    out = _combine(seg, seglen, off, trows, lr, gate, x1, g2, final_norm_g.reshape(1, d), y_buf,
                   tokens_per_batch=t)
    return out.reshape(b, t, d)
```
